```python
import jax, jax.numpy as jnp
from jax import lax
import numpy as np

D_MODEL = 1024
BATCH = 16
SEQ = 4096
DEPTH = 1

CHUNK = 128
SG_HEADS = 8
SG_WIDTH = D_MODEL
SG_HEAD_DIM = SG_WIDTH // SG_HEADS
CV_WIDTH = D_MODEL
CV_K = 31
FFN_DIM = 2816
FFN_K = 3
EPS = 1e-6
IN_COLS = 2 * SG_WIDTH + 2 * CV_WIDTH + 2 * D_MODEL

kernel_name = "hybrid_gmlp_conformer_convffn_encoder"


def rms_norm(x, g):
    xf = x.astype(jnp.float32)
    y = xf * lax.rsqrt(jnp.mean(xf * xf, axis=-1, keepdims=True) + EPS)
    return (y * g.astype(jnp.float32)).astype(x.dtype)


def layer_norm(x, g, b):
    xf = x.astype(jnp.float32)
    mu = jnp.mean(xf, axis=-1, keepdims=True)
    var = jnp.mean(jnp.square(xf - mu), axis=-1, keepdims=True)
    y = (xf - mu) * lax.rsqrt(var + EPS)
    return (y * g.astype(jnp.float32) + b.astype(jnp.float32)).astype(x.dtype)


def depthwise_conv(x, w, b):
    k, c = w.shape
    pad = k // 2
    y = lax.conv_general_dilated(
        x, w[:, None, :].astype(x.dtype), window_strides=(1,), padding=[(pad, pad)],
        dimension_numbers=("NWC", "WIO", "NWC"), feature_group_count=c)
    return y + b.astype(x.dtype)


def spatial_gating(u, v, ln_g, ln_b, w_s, b_s):
    bsz, s, _ = v.shape
    v = layer_norm(v, ln_g, ln_b)
    vc = v.reshape(bsz, s // CHUNK, CHUNK, SG_HEADS, SG_HEAD_DIM)
    mixed = jnp.einsum("hij,bnjhd->bnihd", w_s, vc) + b_s.T[:, :, None]
    return u * mixed.reshape(bsz, s, SG_WIDTH)


def conformer_conv(a, g, dw_w, dw_b, ln_g, ln_b):
    h = a * jax.nn.sigmoid(g)
    h = depthwise_conv(h, dw_w, dw_b)
    h = layer_norm(h, ln_g, ln_b)
    return jax.nn.silu(h)


def setup_inputs(seed: int = 0) -> dict:
    key = jax.random.key(seed)
    ks = jax.random.split(key, 20)
    f32 = jnp.float32
    L, D = DEPTH, D_MODEL

    def nrm(k, shape, scale):
        return jax.random.normal(k, shape, f32) * scale

    return {
        "x": jax.random.normal(ks[0], (BATCH, SEQ, D), f32),
        "norm1_g": 1.0 + nrm(ks[1], (L, D), 0.02),
        "w_in": nrm(ks[2], (L, D, IN_COLS), D ** -0.5),
        "sg_ln_g": 1.0 + nrm(ks[3], (L, SG_WIDTH), 0.02),
        "sg_ln_b": nrm(ks[4], (L, SG_WIDTH), 0.02),
        "sg_w": nrm(ks[5], (L, SG_HEADS, CHUNK, CHUNK), CHUNK ** -0.5),
        "sg_b": 1.0 + nrm(ks[6], (L, SG_HEADS, CHUNK), 0.1),
        "w_a_out": nrm(ks[7], (L, SG_WIDTH, D), SG_WIDTH ** -0.5),
        "cv_dw_w": nrm(ks[8], (L, CV_K, CV_WIDTH), CV_K ** -0.5),
        "cv_dw_b": nrm(ks[9], (L, CV_WIDTH), 0.02),
        "cv_ln_g": 1.0 + nrm(ks[10], (L, CV_WIDTH), 0.02),
        "cv_ln_b": nrm(ks[11], (L, CV_WIDTH), 0.02),
        "w_b_out": nrm(ks[12], (L, CV_WIDTH, D), CV_WIDTH ** -0.5),
        "w_o": nrm(ks[13], (L, D, D), D ** -0.5),
        "norm2_g": 1.0 + nrm(ks[14], (L, D), 0.02),
        "w_up": nrm(ks[15], (L, D, 2 * FFN_DIM), D ** -0.5),
        "ffn_dw_w": nrm(ks[16], (L, FFN_K, FFN_DIM), FFN_K ** -0.5),
        "ffn_dw_b": nrm(ks[17], (L, FFN_DIM), 0.02),
        "w_down": nrm(ks[18], (L, FFN_DIM, D), FFN_DIM ** -0.5),
        "final_g": 1.0 + nrm(ks[19], (D,), 0.02),
    }


def reference(x, norm1_g, w_in, sg_ln_g, sg_ln_b, sg_w, sg_b, w_a_out, cv_dw_w, cv_dw_b,
              cv_ln_g, cv_ln_b, w_b_out, w_o, norm2_g, w_up, ffn_dw_w, ffn_dw_b, w_down, final_g):
    split_at = [SG_WIDTH, 2 * SG_WIDTH, 2 * SG_WIDTH + CV_WIDTH,
                2 * SG_WIDTH + 2 * CV_WIDTH, 2 * SG_WIDTH + 2 * CV_WIDTH + D_MODEL]
    h = x
    for l in range(DEPTH):
        n = rms_norm(h, norm1_g[l])
        z = n @ w_in[l]
        z_u, z_v, z_a, z_g, z_ga, z_gb = jnp.split(z, split_at, axis=-1)
        y_a = spatial_gating(jax.nn.gelu(z_u), jax.nn.gelu(z_v), sg_ln_g[l], sg_ln_b[l],
                             sg_w[l], sg_b[l]) @ w_a_out[l]
        y_b = conformer_conv(z_a, z_g, cv_dw_w[l], cv_dw_b[l], cv_ln_g[l], cv_ln_b[l]) @ w_b_out[l]
        merged = jax.nn.sigmoid(z_ga) * y_a + jax.nn.sigmoid(z_gb) * y_b
        h = h + merged @ w_o[l]
        n = rms_norm(h, norm2_g[l])
        gate, val = jnp.split(n @ w_up[l], 2, axis=-1)
        gate = depthwise_conv(gate, ffn_dw_w[l], ffn_dw_b[l])
        h = h + (jax.nn.gelu(gate) * val) @ w_down[l]
    return rms_norm(h, final_g)
```

```python
import functools

import jax
import jax.numpy as jnp
from jax import lax
from jax.experimental import pallas as pl
from jax.experimental.pallas import tpu as pltpu

D_MODEL = 1024
CHUNK = 128
SG_HEADS = 8
SG_HEAD_DIM = 128
CV_K = 31
FFN_DIM = 2816
FFN_K = 3
EPS = 1e-6

LANES = 128
HALO = 16
TILE = 512
ROWS = TILE + 2 * HALO
ROW_BLK = 128
VMEM_LIMIT = 58 * 1024 * 1024

BF16 = jnp.bfloat16
F32 = jnp.float32


def _sigmoid(x):
    return 0.5 * (jnp.tanh(0.5 * x) + 1.0)


def _gelu(x):
    c = 0.7978845608028654
    return 0.5 * x * (1.0 + jnp.tanh(c * (x + 0.044715 * (x * x * x))))


def _rms_rows(xv, g):
    ms = jnp.mean(xv * xv, axis=-1, keepdims=True)
    return xv * lax.rsqrt(ms + EPS) * g


def _layer_norm_rows(xv, g, b):
    mu = jnp.mean(xv, axis=-1, keepdims=True)
    xc = xv - mu
    var = jnp.mean(xc * xc, axis=-1, keepdims=True)
    return xc * lax.rsqrt(var + EPS) * g + b


def _dot(a, b):
    return jnp.dot(a, b, preferred_element_type=F32)


def _mixer_kernel(x_ref, xp_ref, xn_ref, g1_ref, win_ref, sglg_ref, sglb_ref, sgw_ref, sgbt_ref,
                  wa_ref, cvw_ref, cvb_ref, cvlg_ref, cvlb_ref, wb_ref, wo_ref,
                  out_ref,
                  nb, hs, u_s, zs, ya, vn, lhs, *, tiles_per_seq):
    n_chunks = TILE // CHUNK
    j = pl.program_id(0) % tiles_per_seq
    g1 = g1_ref[...]

    nb[0:HALO, :] = _rms_rows(xp_ref[...], g1).astype(BF16)
    nb[HALO + TILE:ROWS, :] = _rms_rows(xn_ref[...], g1).astype(BF16)
    for rb in range(TILE // ROW_BLK):
        r0 = rb * ROW_BLK
        nb[HALO + r0:HALO + r0 + ROW_BLK, :] = _rms_rows(x_ref[r0:r0 + ROW_BLK, :], g1).astype(BF16)

    a0 = 2 * D_MODEL
    g0 = 3 * D_MODEL
    for cb in range(D_MODEL // 256):
        za = _dot(nb[...], win_ref[:, a0 + cb * 256:a0 + (cb + 1) * 256])
        zg = _dot(nb[...], win_ref[:, g0 + cb * 256:g0 + (cb + 1) * 256])
        glu = za * _sigmoid(zg)
        hs[2 * cb, :, :] = glu[:, :LANES]
        hs[2 * cb + 1, :, :] = glu[:, LANES:]

    @pl.when(j == 0)
    def _():
        hs[:, 0:HALO, :] = jnp.zeros((D_MODEL // LANES, HALO, LANES), F32)

    @pl.when(j == tiles_per_seq - 1)
    def _():
        hs[:, HALO + TILE:ROWS, :] = jnp.zeros((D_MODEL // LANES, HALO, LANES), F32)

    u_s[...] = _gelu(_dot(nb[HALO:HALO + TILE, :], win_ref[:, 0:D_MODEL]))
    zs[...] = _dot(nb[HALO:HALO + TILE, :], win_ref[:, D_MODEL:2 * D_MODEL])
    sglg = sglg_ref[...]
    sglb = sglb_ref[...]
    for c in range(n_chunks):
        v = _layer_norm_rows(_gelu(zs[c * CHUNK:(c + 1) * CHUNK, :]), sglg, sglb).astype(BF16)
        for h in range(SG_HEADS):
            col = (h * n_chunks + c) * SG_HEAD_DIM
            vn[:, col:col + SG_HEAD_DIM] = v[:, h * SG_HEAD_DIM:(h + 1) * SG_HEAD_DIM]

    for h in range(SG_HEADS):
        w = n_chunks * SG_HEAD_DIM
        mixed = _dot(sgw_ref[h], vn[:, h * w:(h + 1) * w]) + sgbt_ref[:, h:h + 1]
        for c in range(n_chunks):
            rows = slice(c * CHUNK, (c + 1) * CHUNK)
            cols = slice(h * SG_HEAD_DIM, (h + 1) * SG_HEAD_DIM)
            lhs[rows, cols] = (u_s[rows, cols] * mixed[:, c * SG_HEAD_DIM:(c + 1) * SG_HEAD_DIM]).astype(BF16)
    ya[...] = _dot(lhs[...], wa_ref[...])

    for cb in range(D_MODEL // LANES):
        cols = slice(cb * LANES, (cb + 1) * LANES)
        for rb in range(TILE // ROW_BLK):
            r0 = rb * ROW_BLK
            acc = jnp.zeros((ROW_BLK, LANES), F32)
            for k in range(CV_K):
                acc = acc + hs[cb, pl.ds(r0 + k + HALO - CV_K // 2, ROW_BLK), :] * cvw_ref[k:k + 1, cols]
            zs[r0:r0 + ROW_BLK, cols] = acc + cvb_ref[:, cols]
    cvlg = cvlg_ref[...]
    cvlb = cvlb_ref[...]
    for rb in range(TILE // ROW_BLK):
        rows = slice(rb * ROW_BLK, (rb + 1) * ROW_BLK)
        y = _layer_norm_rows(zs[rows, :], cvlg, cvlb)
        lhs[rows, :] = (y * _sigmoid(y)).astype(BF16)
    zs[...] = _dot(lhs[...], wb_ref[...])

    ga0 = 4 * D_MODEL
    gb0 = 5 * D_MODEL
    ya[...] = _sigmoid(_dot(nb[HALO:HALO + TILE, :], win_ref[:, ga0:ga0 + D_MODEL])) * ya[...]
    lhs[...] = (_sigmoid(_dot(nb[HALO:HALO + TILE, :], win_ref[:, gb0:gb0 + D_MODEL])) * zs[...]
                + ya[...]).astype(BF16)
    out_ref[...] = x_ref[...] + _dot(lhs[...], wo_ref[...])


def _ffn_kernel(h_ref, hp_ref, hn_ref, g2_ref, wup_ref, dww_ref, dwb_ref, wdn_ref, gf_ref,
                out_ref,
                nb, gs, act, *, tiles_per_seq):
    j = pl.program_id(0) % tiles_per_seq
    g2 = g2_ref[...]

    nb[0:HALO, :] = _rms_rows(hp_ref[...], g2).astype(BF16)
    nb[HALO + TILE:ROWS, :] = _rms_rows(hn_ref[...], g2).astype(BF16)
    for rb in range(TILE // ROW_BLK):
        r0 = rb * ROW_BLK
        nb[HALO + r0:HALO + r0 + ROW_BLK, :] = _rms_rows(h_ref[r0:r0 + ROW_BLK, :], g2).astype(BF16)

    first = j == 0
    last = j == tiles_per_seq - 1
    for cb in range(FFN_DIM // 256):
        c0 = cb * 256
        gate = _dot(nb[...], wup_ref[:, c0:c0 + 256])
        gs[0, :, :] = gate[:, :LANES]
        gs[1, :, :] = gate[:, LANES:]

        @pl.when(first)
        def _():
            gs[:, 0:HALO, :] = jnp.zeros((2, HALO, LANES), F32)

        @pl.when(last)
        def _():
            gs[:, HALO + TILE:ROWS, :] = jnp.zeros((2, HALO, LANES), F32)

        val = _dot(nb[HALO:HALO + TILE, :], wup_ref[:, FFN_DIM + c0:FFN_DIM + c0 + 256])
        for half in range(2):
            cols = slice(c0 + half * LANES, c0 + (half + 1) * LANES)
            conv = dwb_ref[:, cols]
            for k in range(FFN_K):
                conv = conv + gs[half, pl.ds(HALO + k - FFN_K // 2, TILE), :] * dww_ref[k:k + 1, cols]
            act[:, cols] = (_gelu(conv) * val[:, half * LANES:(half + 1) * LANES]).astype(BF16)

    hout = h_ref[...] + _dot(act[...], wdn_ref[...])
    out_ref[...] = _rms_rows(hout, gf_ref[...])


def _const_spec(shape):
    zeros = (0,) * len(shape)
    return pl.BlockSpec(shape, lambda i: zeros, pipeline_mode=pl.Buffered(1))


def _tile_specs(n_tiles):
    per_tile = TILE // HALO
    last_blk = n_tiles * per_tile - 1
    main = pl.BlockSpec((TILE, D_MODEL), lambda i: (i, 0))
    prev = pl.BlockSpec((HALO, D_MODEL), lambda i: (jnp.maximum(i * per_tile - 1, 0), 0))
    nxt = pl.BlockSpec((HALO, D_MODEL), lambda i: (jnp.minimum((i + 1) * per_tile, last_blk), 0))
    return main, prev, nxt


def kernel(x, norm1_g, w_in, sg_ln_g, sg_ln_b, sg_w, sg_b, w_a_out, cv_dw_w, cv_dw_b, cv_ln_g,
           cv_ln_b, w_b_out, w_o, norm2_g, w_up, ffn_dw_w, ffn_dw_b, w_down, final_g):
    bsz, seq, d = x.shape
    assert d == D_MODEL and seq % TILE == 0 and w_in.shape[0] == 1
    n_tok = bsz * seq
    n_tiles = n_tok // TILE
    tiles_per_seq = seq // TILE
    n_chunks = TILE // CHUNK
    x2 = x.reshape(n_tok, d)
    main, prev, nxt = _tile_specs(n_tiles)
    row = lambda v: v.reshape(1, -1)
    params = pltpu.CompilerParams(dimension_semantics=("arbitrary",), vmem_limit_bytes=VMEM_LIMIT)

    mixer_consts = [
        row(norm1_g[0]), w_in[0].astype(BF16), row(sg_ln_g[0]), row(sg_ln_b[0]),
        sg_w[0].astype(BF16), sg_b[0].T, w_a_out[0].astype(BF16), cv_dw_w[0], row(cv_dw_b[0]),
        row(cv_ln_g[0]), row(cv_ln_b[0]), w_b_out[0].astype(BF16), w_o[0].astype(BF16),
    ]
    h = pl.pallas_call(
        functools.partial(_mixer_kernel, tiles_per_seq=tiles_per_seq),
        name="mixer",
        grid=(n_tiles,),
        in_specs=[main, prev, nxt] + [_const_spec(c.shape) for c in mixer_consts],
        out_specs=pl.BlockSpec((TILE, D_MODEL), lambda i: (i, 0)),
        out_shape=jax.ShapeDtypeStruct((n_tok, d), F32),
        scratch_shapes=[
            pltpu.VMEM((ROWS, D_MODEL), BF16),
            pltpu.VMEM((D_MODEL // LANES, ROWS, LANES), F32),
            pltpu.VMEM((TILE, D_MODEL), F32),
            pltpu.VMEM((TILE, D_MODEL), F32),
            pltpu.VMEM((TILE, D_MODEL), F32),
            pltpu.VMEM((CHUNK, SG_HEADS * n_chunks * SG_HEAD_DIM), BF16),
            pltpu.VMEM((TILE, D_MODEL), BF16),
        ],
        compiler_params=params,
    )(x2, x2, x2, *mixer_consts)

    ffn_consts = [
        row(norm2_g[0]), w_up[0].astype(BF16), ffn_dw_w[0], row(ffn_dw_b[0]),
        w_down[0].astype(BF16), row(final_g),
    ]
    out = pl.pallas_call(
        functools.partial(_ffn_kernel, tiles_per_seq=tiles_per_seq),
        name="ffn",
        grid=(n_tiles,),
        in_specs=[main, prev, nxt] + [_const_spec(c.shape) for c in ffn_consts],
        out_specs=pl.BlockSpec((TILE, D_MODEL), lambda i: (i, 0)),
        out_shape=jax.ShapeDtypeStruct((n_tok, d), F32),
        scratch_shapes=[
            pltpu.VMEM((ROWS, D_MODEL), BF16),
            pltpu.VMEM((2, ROWS, LANES), F32),
            pltpu.VMEM((TILE, FFN_DIM), BF16),
        ],
        compiler_params=params,
    )(h, h, h, *ffn_consts)
    return out.reshape(bsz, seq, d)
```

```python
import functools

import jax
import jax.numpy as jnp
from jax import lax
from jax.experimental import pallas as pl
from jax.experimental.pallas import tpu as pltpu

D_MODEL = 1024
CHUNK = 128
SG_HEADS = 8
SG_HEAD_DIM = 128
CV_K = 31
FFN_DIM = 2816
FFN_K = 3
EPS = 1e-6

LANES = 128
MXU_N = 256
HALO = 16
TILE = 512
ROWS = TILE + 2 * HALO
ROW_BLK = 128
VMEM_LIMIT = 58 * 1024 * 1024

BF16 = jnp.bfloat16
F32 = jnp.float32


def _sigmoid(x):
    return 0.5 * (jnp.tanh(0.5 * x) + 1.0)


def _gelu(x):
    c = 0.7978845608028654
    return 0.5 * x * (1.0 + jnp.tanh(c * (x + 0.044715 * (x * x * x))))


def _rms_rows(xv, g):
    ms = jnp.mean(xv * xv, axis=-1, keepdims=True)
    return xv * lax.rsqrt(ms + EPS) * g


def _layer_norm_rows(xv, g, b):
    mu = jnp.mean(xv, axis=-1, keepdims=True)
    xc = xv - mu
    var = jnp.mean(xc * xc, axis=-1, keepdims=True)
    return xc * lax.rsqrt(var + EPS) * g + b


def _dot(a, b):
    return jnp.dot(a, b, preferred_element_type=F32)


def _norm_rows_to(nb, main_ref, prev_ref, next_ref, g):
    nb[0:HALO, :] = _rms_rows(prev_ref[...], g).astype(BF16)
    nb[HALO + TILE:ROWS, :] = _rms_rows(next_ref[...], g).astype(BF16)
    for rb in range(TILE // ROW_BLK):
        r0 = rb * ROW_BLK
        nb[HALO + r0:HALO + r0 + ROW_BLK, :] = _rms_rows(main_ref[r0:r0 + ROW_BLK, :], g).astype(BF16)


def _store_slabs(slab_ref, first_slab, val, keep_prev, keep_next):
    for s in range(val.shape[1] // LANES):
        part = val[:, s * LANES:(s + 1) * LANES]
        slab_ref[first_slab + s, 0:HALO, :] = jnp.where(keep_prev, part[0:HALO], 0.0)
        slab_ref[first_slab + s, HALO:HALO + TILE, :] = part[HALO:HALO + TILE]
        slab_ref[first_slab + s, HALO + TILE:ROWS, :] = jnp.where(keep_next, part[HALO + TILE:ROWS], 0.0)


def _mixer_kernel(x_ref, xp_ref, xn_ref, g1_ref, win_ref, sglg_ref, sglb_ref, sgw_ref, sgbt_ref,
                  wa_ref, cvw_ref, cvb_ref, cvlg_ref, cvlb_ref, wb_ref, wo_ref,
                  out_ref,
                  nb, hs, u_s, v_s, sga, sgb, cv, ya, vn, lhs, lhs2, *, tiles_per_seq):
    n_chunks = TILE // CHUNK
    n_blk = D_MODEL // MXU_N
    j = pl.program_id(0) % tiles_per_seq
    keep_prev = j != 0
    keep_next = j != tiles_per_seq - 1

    _norm_rows_to(nb, x_ref, xp_ref, xn_ref, g1_ref[...])
    nb_main = lambda: nb[HALO:HALO + TILE, :]

    a0 = 2 * D_MODEL
    g0 = 3 * D_MODEL
    for b in range(n_blk):
        za = _dot(nb[...], win_ref[:, a0 + b * MXU_N:a0 + (b + 1) * MXU_N])
        zg = _dot(nb[...], win_ref[:, g0 + b * MXU_N:g0 + (b + 1) * MXU_N])
        _store_slabs(hs, b * (MXU_N // LANES), za * _sigmoid(zg), keep_prev, keep_next)

    def conv_block(cb):
        cols = slice(cb * LANES, (cb + 1) * LANES)
        for rb in range(TILE // ROW_BLK):
            r0 = rb * ROW_BLK
            acc = jnp.zeros((ROW_BLK, LANES), F32)
            for k in range(CV_K):
                acc = acc + hs[cb, pl.ds(r0 + k + HALO - CV_K // 2, ROW_BLK), :] * cvw_ref[k:k + 1, cols]
            cv[r0:r0 + ROW_BLK, cols] = acc + cvb_ref[:, cols]

    def v_norm(c):
        v = _layer_norm_rows(v_s[c * CHUNK:(c + 1) * CHUNK, :], sglg_ref[...], sglb_ref[...]).astype(BF16)
        for h in range(SG_HEADS):
            col = (h * n_chunks + c) * SG_HEAD_DIM
            vn[:, col:col + SG_HEAD_DIM] = v[:, h * SG_HEAD_DIM:(h + 1) * SG_HEAD_DIM]

    def gate_head(h):
        w = n_chunks * SG_HEAD_DIM
        mixed = _dot(sgw_ref[h], vn[:, h * w:(h + 1) * w]) + sgbt_ref[:, h:h + 1]
        for c in range(n_chunks):
            rows = slice(c * CHUNK, (c + 1) * CHUNK)
            cols = slice(h * SG_HEAD_DIM, (h + 1) * SG_HEAD_DIM)
            lhs[rows, cols] = (u_s[rows, cols] * mixed[:, c * SG_HEAD_DIM:(c + 1) * SG_HEAD_DIM]).astype(BF16)

    def proj_piece(dst, col0, b, act):
        blk = slice(b * MXU_N, (b + 1) * MXU_N)
        dst[:, blk] = act(_dot(nb_main(), win_ref[:, col0 + b * MXU_N:col0 + (b + 1) * MXU_N]))

    for b in range(n_blk):
        proj_piece(v_s, D_MODEL, b, _gelu)
        if b % 2 == 1:
            conv_block(b // 2)
    for b in range(n_blk):
        proj_piece(u_s, 0, b, _gelu)
        v_norm(b)
        if b % 2 == 1:
            conv_block(2 + b // 2)
    for b in range(n_blk):
        proj_piece(sga, 4 * D_MODEL, b, _sigmoid)
        gate_head(b)
        if b % 2 == 1:
            conv_block(4 + b // 2)
    for b in range(n_blk):
        proj_piece(sgb, 5 * D_MODEL, b, _sigmoid)
        gate_head(n_blk + b)
        if b % 2 == 1:
            conv_block(6 + b // 2)

    ya[...] = _dot(lhs[...], wa_ref[...])
    for rb in range(TILE // ROW_BLK):
        rows = slice(rb * ROW_BLK, (rb + 1) * ROW_BLK)
        y = _layer_norm_rows(cv[rows, :], cvlg_ref[...], cvlb_ref[...])
        lhs2[rows, :] = (y * _sigmoid(y)).astype(BF16)

    for b in range(n_blk):
        blk = slice(b * MXU_N, (b + 1) * MXU_N)
        yb = _dot(lhs2[...], wb_ref[:, blk])
        lhs[:, blk] = (sga[:, blk] * ya[:, blk] + sgb[:, blk] * yb).astype(BF16)
    out_ref[...] = x_ref[...] + _dot(lhs[...], wo_ref[...])


def _ffn_kernel(h_ref, hp_ref, hn_ref, g2_ref, wup_ref, dww_ref, dwb_ref, wdn_ref, gf_ref,
                out_ref,
                nb, gs, act, *, tiles_per_seq):
    j = pl.program_id(0) % tiles_per_seq
    keep_prev = j != 0
    keep_next = j != tiles_per_seq - 1

    _norm_rows_to(nb, h_ref, hp_ref, hn_ref, g2_ref[...])

    n_blk = FFN_DIM // MXU_N
    slabs = MXU_N // LANES
    for b in range(n_blk):
        c0 = b * MXU_N
        gate = _dot(nb[...], wup_ref[:, c0:c0 + MXU_N])
        _store_slabs(gs, b * slabs, gate, keep_prev, keep_next)
        val = _dot(nb[HALO:HALO + TILE, :], wup_ref[:, FFN_DIM + c0:FFN_DIM + c0 + MXU_N])
        for s in range(slabs):
            cols = slice(c0 + s * LANES, c0 + (s + 1) * LANES)
            conv = dwb_ref[:, cols]
            for k in range(FFN_K):
                conv = conv + gs[b * slabs + s, pl.ds(HALO + k - FFN_K // 2, TILE), :] * dww_ref[k:k + 1, cols]
            act[:, cols] = (_gelu(conv) * val[:, s * LANES:(s + 1) * LANES]).astype(BF16)

    hout = h_ref[...] + _dot(act[...], wdn_ref[...])
    out_ref[...] = _rms_rows(hout, gf_ref[...])


def _const_spec(shape):
    zeros = (0,) * len(shape)
    return pl.BlockSpec(shape, lambda i: zeros, pipeline_mode=pl.Buffered(1))


def _tile_specs(n_tiles):
    per_tile = TILE // HALO
    last_blk = n_tiles * per_tile - 1
    main = pl.BlockSpec((TILE, D_MODEL), lambda i: (i, 0))
    prev = pl.BlockSpec((HALO, D_MODEL), lambda i: (jnp.maximum(i * per_tile - 1, 0), 0))
    nxt = pl.BlockSpec((HALO, D_MODEL), lambda i: (jnp.minimum((i + 1) * per_tile, last_blk), 0))
    return main, prev, nxt


def kernel(x, norm1_g, w_in, sg_ln_g, sg_ln_b, sg_w, sg_b, w_a_out, cv_dw_w, cv_dw_b, cv_ln_g,
           cv_ln_b, w_b_out, w_o, norm2_g, w_up, ffn_dw_w, ffn_dw_b, w_down, final_g):
    bsz, seq, d = x.shape
    assert d == D_MODEL and seq % TILE == 0 and w_in.shape[0] == 1
    n_tok = bsz * seq
    n_tiles = n_tok // TILE
    tiles_per_seq = seq // TILE
    n_chunks = TILE // CHUNK
    x2 = x.reshape(n_tok, d)
    main, prev, nxt = _tile_specs(n_tiles)
    row = lambda v: v.reshape(1, -1)
    params = pltpu.CompilerParams(dimension_semantics=("arbitrary",), vmem_limit_bytes=VMEM_LIMIT)
    tile_f32 = pltpu.VMEM((TILE, D_MODEL), F32)
    tile_bf16 = pltpu.VMEM((TILE, D_MODEL), BF16)

    mixer_consts = [
        row(norm1_g[0]), w_in[0].astype(BF16), row(sg_ln_g[0]), row(sg_ln_b[0]),
        sg_w[0].astype(BF16), sg_b[0].T, w_a_out[0].astype(BF16), cv_dw_w[0], row(cv_dw_b[0]),
        row(cv_ln_g[0]), row(cv_ln_b[0]), w_b_out[0].astype(BF16), w_o[0].astype(BF16),
    ]
    h = pl.pallas_call(
        functools.partial(_mixer_kernel, tiles_per_seq=tiles_per_seq),
        name="mixer",
        grid=(n_tiles,),
        in_specs=[main, prev, nxt] + [_const_spec(c.shape) for c in mixer_consts],
        out_specs=pl.BlockSpec((TILE, D_MODEL), lambda i: (i, 0)),
        out_shape=jax.ShapeDtypeStruct((n_tok, d), F32),
        scratch_shapes=[
            pltpu.VMEM((ROWS, D_MODEL), BF16),
            pltpu.VMEM((D_MODEL // LANES, ROWS, LANES), F32),
            tile_f32,
            tile_f32,
            tile_f32,
            tile_f32,
            tile_f32,
            tile_f32,
            pltpu.VMEM((CHUNK, SG_HEADS * n_chunks * SG_HEAD_DIM), BF16),
            tile_bf16,
            tile_bf16,
        ],
        compiler_params=params,
    )(x2, x2, x2, *mixer_consts)

    ffn_consts = [
        row(norm2_g[0]), w_up[0].astype(BF16), ffn_dw_w[0], row(ffn_dw_b[0]),
        w_down[0].astype(BF16), row(final_g),
    ]
    out = pl.pallas_call(
        functools.partial(_ffn_kernel, tiles_per_seq=tiles_per_seq),
        name="ffn",
        grid=(n_tiles,),
        in_specs=[main, prev, nxt] + [_const_spec(c.shape) for c in ffn_consts],
        out_specs=pl.BlockSpec((TILE, D_MODEL), lambda i: (i, 0)),
        out_shape=jax.ShapeDtypeStruct((n_tok, d), F32),
        scratch_shapes=[
            pltpu.VMEM((ROWS, D_MODEL), BF16),
            pltpu.VMEM((FFN_DIM // LANES, ROWS, LANES), F32),
            pltpu.VMEM((TILE, FFN_DIM), BF16),
        ],
        compiler_params=params,
    )(h, h, h, *ffn_consts)
    return out.reshape(bsz, seq, d)
```
